```python
import jax, jax.numpy as jnp
from jax import lax
import numpy as np

D_MODEL = 2048
BATCH = 4
SEQ = 2048
DEPTH = 1
DEC_BATCH = 128
DEC_SEQ = 8
PAST_LEN = 16384
PAGE_SIZE = 128

D_MIX = D_MODEL
W_A = D_MIX // 2
W_B = D_MIX - W_A
N_HEADS_A = 8
HB_A = W_A // N_HEADS_A
N_GROUPS_B = 16
CONV_A = 4
CONV_B = 31
LRU_C = 8.0
D_IN = 2 * W_A + 3 * W_B
EPS = 1e-6

kernel_name = "hymba_style_rglru_conformer_conv_step"


def _rmsnorm(x, gain):
    x32 = x.astype(jnp.float32)
    return x32 * lax.rsqrt(jnp.mean(x32 * x32, axis=-1, keepdims=True) + EPS) * gain.astype(jnp.float32)


def _layernorm(x, gain, bias):
    mu = jnp.mean(x, axis=-1, keepdims=True)
    xc = x - mu
    var = jnp.mean(xc * xc, axis=-1, keepdims=True)
    return xc * lax.rsqrt(var + EPS) * gain.astype(jnp.float32) + bias.astype(jnp.float32)


def _dwconv_valid(x_pad, w, b):
    c = x_pad.shape[-1]
    out = lax.conv_general_dilated(
        x_pad, w.astype(jnp.float32)[:, None, :], window_strides=(1,), padding='VALID',
        dimension_numbers=('NWC', 'WIO', 'NWC'), feature_group_count=c)
    return out + b.astype(jnp.float32)


def _lru_combine(left, right):
    a1, b1 = left
    a2, b2 = right
    return a1 * a2, a2 * b1 + b2


def _layer(x, h0, buf_a, buf_b, norm_gain, w_in, conv_a_w, conv_a_b, gate_a_w, gate_a_b,
           gate_x_w, gate_x_b, lru_param, conv_b_w, conv_b_b, ln_b_gain, ln_b_bias, w_out):
    bsz, t = x.shape[0], x.shape[1]
    x32 = x.astype(jnp.float32)
    xn = _rmsnorm(x32, norm_gain)
    proj = jnp.einsum('btd,de->bte', xn, w_in.astype(jnp.float32))
    xa, ga, vb, gglu, gb = jnp.split(
        proj, [W_A, 2 * W_A, 2 * W_A + W_B, 2 * W_A + 2 * W_B], axis=-1)

    xa_pad = jnp.concatenate([buf_a.astype(jnp.float32), xa], axis=1)
    new_buf_a = xa_pad[:, -(CONV_A - 1):]
    xc = _dwconv_valid(xa_pad, conv_a_w, conv_a_b)
    xh = xc.reshape(bsz, t, N_HEADS_A, HB_A)
    r = jax.nn.sigmoid(jnp.einsum('bthi,hij->bthj', xh, gate_a_w.astype(jnp.float32))
                       .reshape(bsz, t, W_A) + gate_a_b.astype(jnp.float32))
    ig = jax.nn.sigmoid(jnp.einsum('bthi,hij->bthj', xh, gate_x_w.astype(jnp.float32))
                        .reshape(bsz, t, W_A) + gate_x_b.astype(jnp.float32))
    log_a = -LRU_C * r * jax.nn.softplus(-lru_param.astype(jnp.float32))
    a = jnp.exp(log_a)
    mult = jnp.sqrt(-jnp.expm1(2.0 * log_a))
    bterm = mult * (ig * xc)
    bterm = bterm.at[:, 0].add(a[:, 0] * h0.astype(jnp.float32))
    _, h = lax.associative_scan(_lru_combine, (a, bterm), axis=1)
    ya = h * jax.nn.silu(ga)

    u = vb * jax.nn.sigmoid(gglu)
    u_pad = jnp.concatenate([buf_b.astype(jnp.float32), u], axis=1)
    new_buf_b = u_pad[:, -(CONV_B - 1):]
    cb = _dwconv_valid(u_pad, conv_b_w, conv_b_b)
    cb = _layernorm(cb, ln_b_gain, ln_b_bias)
    yb = jax.nn.silu(cb) * jax.nn.silu(gb)

    y = jnp.einsum('bte,ed->btd', jnp.concatenate([ya, yb], axis=-1), w_out.astype(jnp.float32))
    return x32 + y, h[:, -1], new_buf_a, new_buf_b


def setup_inputs(seed: int = 0) -> dict:
    key = jax.random.key(seed)
    ks = jax.random.split(key, 24)
    f = jnp.float32
    nrm = lambda k, s, sc: jax.random.normal(k, s, f) * sc
    a_base = jax.random.uniform(ks[12], (DEPTH, W_A), f, 0.9, 0.999)
    return {
        "x_prompt": nrm(ks[0], (BATCH, SEQ, D_MODEL), 1.0),
        "x_sample": nrm(ks[1], (DEC_BATCH, DEC_SEQ, D_MODEL), 1.0),
        "state_lru_h": nrm(ks[2], (DEPTH, DEC_BATCH, W_A), 0.5),
        "state_lru_conv": nrm(ks[3], (DEPTH, DEC_BATCH, CONV_A - 1, W_A), 0.5),
        "state_glu_conv": nrm(ks[4], (DEPTH, DEC_BATCH, CONV_B - 1, W_B), 0.5),
        "norm_gain": 1.0 + nrm(ks[5], (DEPTH, D_MODEL), 0.02),
        "w_in": nrm(ks[6], (DEPTH, D_MODEL, D_IN), D_MODEL ** -0.5),
        "conv_a_w": nrm(ks[7], (DEPTH, CONV_A, W_A), CONV_A ** -0.5),
        "conv_a_b": nrm(ks[8], (DEPTH, W_A), 0.01),
        "gate_a_w": nrm(ks[9], (DEPTH, N_HEADS_A, HB_A, HB_A), HB_A ** -0.5),
        "gate_a_b": nrm(ks[10], (DEPTH, W_A), 0.01),
        "gate_x_w": nrm(ks[11], (DEPTH, N_HEADS_A, HB_A, HB_A), HB_A ** -0.5),
        "gate_x_b": nrm(ks[13], (DEPTH, W_A), 0.01),
        "lru_param": jnp.log(a_base) - jnp.log1p(-a_base),
        "conv_b_w": nrm(ks[14], (DEPTH, CONV_B, W_B), CONV_B ** -0.5),
        "conv_b_b": nrm(ks[15], (DEPTH, W_B), 0.01),
        "ln_b_gain": 1.0 + nrm(ks[16], (DEPTH, W_B), 0.02),
        "ln_b_bias": nrm(ks[17], (DEPTH, W_B), 0.01),
        "w_out": nrm(ks[18], (DEPTH, D_MIX, D_MODEL), D_MIX ** -0.5),
        "final_gain": 1.0 + nrm(ks[19], (D_MODEL,), 0.02),
    }


def reference(x_prompt, x_sample, state_lru_h, state_lru_conv, state_glu_conv,
              norm_gain, w_in, conv_a_w, conv_a_b, gate_a_w, gate_a_b, gate_x_w, gate_x_b,
              lru_param, conv_b_w, conv_b_b, ln_b_gain, ln_b_bias, w_out, final_gain):
    bp = x_prompt.shape[0]
    sdt = state_lru_h.dtype
    hp = x_prompt.astype(jnp.float32)
    hs = x_sample.astype(jnp.float32)
    ph, pca, pcb, sh, sca, scb = [], [], [], [], [], []
    for l in range(DEPTH):
        w = (norm_gain[l], w_in[l], conv_a_w[l], conv_a_b[l], gate_a_w[l], gate_a_b[l],
             gate_x_w[l], gate_x_b[l], lru_param[l], conv_b_w[l], conv_b_b[l],
             ln_b_gain[l], ln_b_bias[l], w_out[l])
        hp, h_p, ca_p, cb_p = _layer(
            hp, jnp.zeros((bp, W_A), jnp.float32), jnp.zeros((bp, CONV_A - 1, W_A), jnp.float32),
            jnp.zeros((bp, CONV_B - 1, W_B), jnp.float32), *w)
        hs, h_s, ca_s, cb_s = _layer(hs, state_lru_h[l], state_lru_conv[l], state_glu_conv[l], *w)
        ph.append(h_p); pca.append(ca_p); pcb.append(cb_p)
        sh.append(h_s); sca.append(ca_s); scb.append(cb_s)
    y_prompt = _rmsnorm(hp, final_gain).astype(x_prompt.dtype)
    y_sample = _rmsnorm(hs, final_gain).astype(x_sample.dtype)
    new_lru_h_prompt = jnp.stack(ph).astype(sdt)
    new_lru_conv_prompt = jnp.stack(pca).astype(sdt)
    new_glu_conv_prompt = jnp.stack(pcb).astype(sdt)
    new_lru_h_sample = jnp.stack(sh).astype(sdt)
    new_lru_conv_sample = jnp.stack(sca).astype(sdt)
    new_glu_conv_sample = jnp.stack(scb).astype(sdt)
    return (y_prompt, y_sample, new_lru_h_prompt, new_lru_conv_prompt, new_glu_conv_prompt,
            new_lru_h_sample, new_lru_conv_sample, new_glu_conv_sample)
```

```python
import collections
import functools

import jax
import jax.numpy as jnp
from jax import lax
from jax.experimental import pallas as pl
from jax.experimental.pallas import tpu as pltpu

W_A = 1024
W_B = 1024
N_HEADS_A = 8
HB_A = W_A // N_HEADS_A
CONV_A = 4
CONV_B = 31
LRU_C = 8.0
EPS = 1e-6

SUBLANES = 8
PROMPT_ROWS = 256
PROMPT_CHUNK = 32
SAMPLE_SEQS = 16
VMEM_LIMIT_BYTES = 58 * 1024 * 1024

_Weights = collections.namedtuple(
    "_Weights", "ng win caw cab wg gab gxb lam cbw cbb lng lnb wout fg")
_Scratch = collections.namedtuple(
    "_Scratch", "xn xa_pad ga u_pad gb xc xcb gr gi y yo h")


def _round_up(x, m):
    return (x + m - 1) // m * m


def _dot(a, b):
    return jnp.dot(a, b, preferred_element_type=jnp.float32)


def _silu(x):
    return x * jax.nn.sigmoid(x)


def _causal_conv(pad_ref, taps, bias, base, step_rows, rows):
    n_taps = len(taps)
    groups = {}
    for k in range(n_taps):
        off = base - (n_taps - 1 - k) * step_rows
        groups.setdefault(off % SUBLANES, []).append((off, k))
    acc = None
    for res in sorted(groups):
        part = None
        for off, k in groups[res]:
            v = pad_ref[pl.ds(off, rows), :] * taps[k]
            part = v if part is None else part + v
        acc = part if acc is None else acc + part
    return acc + bias


def _scan_rows(a, b, h_b):
    rows, width = a.shape
    n_tiles = rows // SUBLANES
    a3 = a.reshape(n_tiles, SUBLANES, width)
    b3 = b.reshape(n_tiles, SUBLANES, width)
    row = lax.broadcasted_iota(jnp.int32, a3.shape, 1)
    d = 1
    while d < SUBLANES:
        keep = row >= d
        a_prev = jnp.where(keep, pltpu.roll(a3, d, axis=1), 1.0)
        b_prev = jnp.where(keep, pltpu.roll(b3, d, axis=1), 0.0)
        b3 = a3 * b_prev + b3
        a3 = a3 * a_prev
        d *= 2
    outs = []
    for g in range(n_tiles):
        h_g = a3[g] * h_b + b3[g]
        outs.append(h_g)
        h_b = jnp.broadcast_to(h_g[SUBLANES - 1:SUBLANES, :], (SUBLANES, width))
    return jnp.concatenate(outs, axis=0), h_b


def _compute_tile(step_rows, rows, rc, pad_a, pad_b, xget, yput, w, s):
    n_chunks = rows // rc
    bf16 = jnp.bfloat16

    ng = w.ng[...]
    for c in range(n_chunks):
        x = xget(c)
        ms = jnp.mean(x * x, axis=-1, keepdims=True)
        s.xn[c * rc:(c + 1) * rc, :] = (x * lax.rsqrt(ms + EPS) * ng).astype(bf16)

    xn = s.xn[...]
    s.xa_pad[pad_a:pad_a + rows, :] = _dot(xn, w.win[:, 0:W_A])
    s.ga[...] = _dot(xn, w.win[:, W_A:2 * W_A])
    s.u_pad[pad_b:pad_b + rows, :] = _dot(xn, w.win[:, 2 * W_A:2 * W_A + W_B])
    s.gr[...] = _dot(xn, w.win[:, 2 * W_A + W_B:2 * W_A + 2 * W_B])
    s.gb[...] = _dot(xn, w.win[:, 2 * W_A + 2 * W_B:2 * W_A + 3 * W_B])

    taps_a = [jnp.broadcast_to(w.caw[k:k + 1, :], (rc, W_A)) for k in range(CONV_A)]
    cab = w.cab[...]
    for c in range(n_chunks):
        lo = c * rc
        u_rows = pl.ds(pad_b + lo, rc)
        s.u_pad[u_rows, :] = s.u_pad[u_rows, :] * jax.nn.sigmoid(s.gr[lo:lo + rc, :])
        xc = _causal_conv(s.xa_pad, taps_a, cab, pad_a + lo, step_rows, rc)
        s.xc[lo:lo + rc, :] = xc
        s.xcb[lo:lo + rc, :] = xc.astype(bf16)

    for hd in range(N_HEADS_A):
        cols = slice(hd * HB_A, (hd + 1) * HB_A)
        g = _dot(s.xcb[:, cols], w.wg[hd])
        s.gr[:, cols] = g[:, :HB_A]
        s.gi[:, cols] = g[:, HB_A:]

    lam = w.lam[...]
    neg_c_softplus = -LRU_C * (jnp.maximum(-lam, 0.0) + jnp.log1p(jnp.exp(-jnp.abs(lam))))
    gab = w.gab[...]
    gxb = w.gxb[...]
    taps_b = [jnp.broadcast_to(w.cbw[k:k + 1, :], (rc, W_B)) for k in range(CONV_B)]
    cbb = w.cbb[...]
    lng = w.lng[...]
    lnb = w.lnb[...]
    h_carry = s.h[...]
    for c in range(n_chunks):
        lo = c * rc
        sl = slice(lo, lo + rc)
        xc = s.xc[sl, :]
        r = jax.nn.sigmoid(s.gr[sl, :] + gab)
        ig = jax.nn.sigmoid(s.gi[sl, :] + gxb)
        a = jnp.exp(r * neg_c_softplus)
        b = jnp.sqrt(1.0 - a * a) * (ig * xc)
        if step_rows == 1:
            h, h_carry = _scan_rows(a, b, h_carry)
        else:
            h = a * h_carry + b
            h_carry = h
        s.y[sl, 0:W_A] = (h * _silu(s.ga[sl, :])).astype(bf16)
        cb = _causal_conv(s.u_pad, taps_b, cbb, pad_b + lo, step_rows, rc)
        mu = jnp.mean(cb, axis=-1, keepdims=True)
        cen = cb - mu
        var = jnp.mean(cen * cen, axis=-1, keepdims=True)
        cb = cen * lax.rsqrt(var + EPS) * lng + lnb
        s.y[sl, W_A:W_A + W_B] = (_silu(cb) * _silu(s.gb[sl, :])).astype(bf16)
    s.h[...] = h_carry

    s.yo[...] = _dot(s.y[...], w.wout[...])
    fg = w.fg[...]
    for c in range(n_chunks):
        z = xget(c) + s.yo[c * rc:(c + 1) * rc, :]
        ms = jnp.mean(z * z, axis=-1, keepdims=True)
        yput(c, z * lax.rsqrt(ms + EPS) * fg)


def _prompt_kernel(n_tiles, x_ref, *refs):
    w = _Weights(*refs[:14])
    y_ref, h_ref, ca_ref, cb_ref = refs[14:18]
    s = _Scratch(*refs[18:])
    rows, rc = PROMPT_ROWS, PROMPT_CHUNK
    pad_a = _round_up(CONV_A - 1, SUBLANES)
    pad_b = _round_up(CONV_B - 1, SUBLANES)
    tt = pl.program_id(1)

    @pl.when(tt == 0)
    def _():
        s.xa_pad[0:pad_a, :] = jnp.zeros((pad_a, W_A), jnp.float32)
        s.u_pad[0:pad_b, :] = jnp.zeros((pad_b, W_B), jnp.float32)
        s.h[...] = jnp.zeros(s.h.shape, jnp.float32)

    def xget(c):
        return x_ref[0, c * rc:(c + 1) * rc, :]

    def yput(c, val):
        y_ref[0, c * rc:(c + 1) * rc, :] = val

    _compute_tile(1, rows, rc, pad_a, pad_b, xget, yput, w, s)

    @pl.when(tt == n_tiles - 1)
    def _():
        h_ref[0] = s.h[0:1, :]
        ca_ref[0] = s.xa_pad[pad_a + rows - (CONV_A - 1):pad_a + rows, :]
        cb_ref[0] = s.u_pad[pad_b + rows - (CONV_B - 1):pad_b + rows, :]

    s.xa_pad[0:pad_a, :] = s.xa_pad[rows:rows + pad_a, :]
    s.u_pad[0:pad_b, :] = s.u_pad[rows:rows + pad_b, :]


def _sample_kernel(n_steps, x_ref, h0_ref, ca0_ref, cb0_ref, *refs):
    w = _Weights(*refs[:14])
    y_ref, h_ref, ca_ref, cb_ref = refs[14:18]
    s = _Scratch(*refs[18:])
    rs = SAMPLE_SEQS
    rows = n_steps * rs
    d_model = x_ref.shape[1] // n_steps
    pad_a = (CONV_A - 1) * rs
    pad_b = (CONV_B - 1) * rs

    for j in range(CONV_A - 1):
        s.xa_pad[j * rs:(j + 1) * rs, :] = ca0_ref[:, j * W_A:(j + 1) * W_A]
    for j in range(CONV_B - 1):
        s.u_pad[j * rs:(j + 1) * rs, :] = cb0_ref[:, j * W_B:(j + 1) * W_B]
    s.h[...] = h0_ref[...]

    def xget(c):
        return x_ref[:, c * d_model:(c + 1) * d_model]

    def yput(c, val):
        y_ref[:, c * d_model:(c + 1) * d_model] = val

    _compute_tile(rs, rows, rs, pad_a, pad_b, xget, yput, w, s)

    h_ref[...] = s.h[...]
    for j in range(CONV_A - 1):
        ca_ref[:, j * W_A:(j + 1) * W_A] = s.xa_pad[(n_steps + j) * rs:(n_steps + j + 1) * rs, :]
    for j in range(CONV_B - 1):
        cb_ref[:, j * W_B:(j + 1) * W_B] = s.u_pad[(n_steps + j) * rs:(n_steps + j + 1) * rs, :]


def _weight_specs(weights, n_grid_axes):
    def spec(arr):
        zeros = (0,) * arr.ndim
        if n_grid_axes == 1:
            index_map = lambda i: zeros
        else:
            index_map = lambda i, j: zeros
        return pl.BlockSpec(arr.shape, index_map, pipeline_mode=pl.Buffered(1))
    return [spec(a) for a in weights]


def _scratch_shapes(rows, pad_a, pad_b, d_model, h_rows):
    f32, bf16 = jnp.float32, jnp.bfloat16
    return [
        pltpu.VMEM((rows, d_model), bf16),
        pltpu.VMEM((pad_a + rows, W_A), f32),
        pltpu.VMEM((rows, W_A), f32),
        pltpu.VMEM((pad_b + rows, W_B), f32),
        pltpu.VMEM((rows, W_B), f32),
        pltpu.VMEM((rows, W_A), f32),
        pltpu.VMEM((rows, W_A), bf16),
        pltpu.VMEM((rows, W_A), f32),
        pltpu.VMEM((rows, W_A), f32),
        pltpu.VMEM((rows, W_A + W_B), bf16),
        pltpu.VMEM((rows, d_model), f32),
        pltpu.VMEM((h_rows, W_A), f32),
    ]


def kernel(x_prompt, x_sample, state_lru_h, state_lru_conv, state_glu_conv, norm_gain, w_in, conv_a_w, conv_a_b, gate_a_w, gate_a_b, gate_x_w, gate_x_b, lru_param, conv_b_w, conv_b_b, ln_b_gain, ln_b_bias, w_out, final_gain):
    depth = w_in.shape[0]
    assert depth == 1, "single-layer trunk only"
    bp, seq, d_model = x_prompt.shape
    bs, n_steps, _ = x_sample.shape
    assert seq % PROMPT_ROWS == 0 and bs % SAMPLE_SEQS == 0
    f32, bf16 = jnp.float32, jnp.bfloat16
    sdt = state_lru_h.dtype

    row = lambda v: v.reshape(1, -1).astype(f32)
    gate_w = jnp.concatenate([gate_a_w[0], gate_x_w[0]], axis=-1).astype(bf16)
    weights = (
        row(norm_gain[0]), w_in[0].astype(bf16), conv_a_w[0].astype(f32), row(conv_a_b[0]),
        gate_w, row(gate_a_b[0]), row(gate_x_b[0]), row(lru_param[0]),
        conv_b_w[0].astype(f32), row(conv_b_b[0]), row(ln_b_gain[0]), row(ln_b_bias[0]),
        w_out[0].astype(bf16), row(final_gain),
    )

    n_tiles = seq // PROMPT_ROWS
    pad_a = _round_up(CONV_A - 1, SUBLANES)
    pad_b = _round_up(CONV_B - 1, SUBLANES)
    y_p, h_p, ca_p, cb_p = pl.pallas_call(
        functools.partial(_prompt_kernel, n_tiles),
        grid=(bp, n_tiles),
        in_specs=[pl.BlockSpec((1, PROMPT_ROWS, d_model), lambda b, t: (b, t, 0))]
        + _weight_specs(weights, 2),
        out_specs=[
            pl.BlockSpec((1, PROMPT_ROWS, d_model), lambda b, t: (b, t, 0)),
            pl.BlockSpec((1, 1, W_A), lambda b, t: (b, 0, 0)),
            pl.BlockSpec((1, CONV_A - 1, W_A), lambda b, t: (b, 0, 0)),
            pl.BlockSpec((1, CONV_B - 1, W_B), lambda b, t: (b, 0, 0)),
        ],
        out_shape=[
            jax.ShapeDtypeStruct((bp, seq, d_model), x_prompt.dtype),
            jax.ShapeDtypeStruct((bp, 1, W_A), sdt),
            jax.ShapeDtypeStruct((bp, CONV_A - 1, W_A), sdt),
            jax.ShapeDtypeStruct((bp, CONV_B - 1, W_B), sdt),
        ],
        scratch_shapes=_scratch_shapes(PROMPT_ROWS, pad_a, pad_b, d_model, SUBLANES),
        compiler_params=pltpu.CompilerParams(
            dimension_semantics=("arbitrary", "arbitrary"),
            vmem_limit_bytes=VMEM_LIMIT_BYTES),
        name="prompt_layer",
    )(x_prompt.astype(f32), *weights)

    rs = SAMPLE_SEQS
    rows = n_steps * rs
    seq_block = lambda width: pl.BlockSpec((rs, width), lambda i: (i, 0))
    y_s, h_s, ca_s, cb_s = pl.pallas_call(
        functools.partial(_sample_kernel, n_steps),
        grid=(bs // rs,),
        in_specs=[
            seq_block(n_steps * d_model), seq_block(W_A),
            seq_block((CONV_A - 1) * W_A), seq_block((CONV_B - 1) * W_B),
        ] + _weight_specs(weights, 1),
        out_specs=[
            seq_block(n_steps * d_model), seq_block(W_A),
            seq_block((CONV_A - 1) * W_A), seq_block((CONV_B - 1) * W_B),
        ],
        out_shape=[
            jax.ShapeDtypeStruct((bs, n_steps * d_model), x_sample.dtype),
            jax.ShapeDtypeStruct((bs, W_A), sdt),
            jax.ShapeDtypeStruct((bs, (CONV_A - 1) * W_A), sdt),
            jax.ShapeDtypeStruct((bs, (CONV_B - 1) * W_B), sdt),
        ],
        scratch_shapes=_scratch_shapes(
            rows, (CONV_A - 1) * rs, (CONV_B - 1) * rs, d_model, rs),
        compiler_params=pltpu.CompilerParams(
            dimension_semantics=("arbitrary",),
            vmem_limit_bytes=VMEM_LIMIT_BYTES),
        name="sample_layer",
    )(
        x_sample.astype(f32).reshape(bs, n_steps * d_model),
        state_lru_h[0].astype(f32),
        state_lru_conv[0].astype(f32).reshape(bs, (CONV_A - 1) * W_A),
        state_glu_conv[0].astype(f32).reshape(bs, (CONV_B - 1) * W_B),
        *weights,
    )

    return (
        y_p,
        y_s.reshape(bs, n_steps, d_model),
        h_p.reshape(1, bp, W_A),
        ca_p.reshape(1, bp, CONV_A - 1, W_A),
        cb_p.reshape(1, bp, CONV_B - 1, W_B),
        h_s.reshape(1, bs, W_A),
        ca_s.reshape(1, bs, CONV_A - 1, W_A),
        cb_s.reshape(1, bs, CONV_B - 1, W_B),
    )
```
